```python
import math
import jax, jax.numpy as jnp
from jax import lax
import numpy as np

D_MODEL = 1024
BATCH = 8
SEQ = 2048
DEPTH = 2
DEC_BATCH = 128
DEC_SEQ = 8
PAST_LEN = 16384
PAGE_SIZE = 128

N_META = 16
SSD_EXPAND = 2
D_INNER = SSD_EXPAND * D_MODEL
HEAD_DIM = 64
N_HEADS = D_INNER // HEAD_DIM
N_GROUPS = 4
D_STATE = 128
CONV_K = 4
CONV_DIM = D_INNER + 2 * N_GROUPS * D_STATE
CHUNK = 128
POOL_WINDOWS = (2, 4, 8, 16)
N_POOL_GROUPS = 4
POOL_DIM = D_MODEL
POOL_GROUP_DIM = POOL_DIM // N_POOL_GROUPS
MAX_WIN = 16
D_FF = ((8 * D_MODEL // 3 + 127) // 128) * 128
IN_COLS = D_INNER + CONV_DIM + N_HEADS + POOL_DIM + 2 * D_MODEL
EPS = 1e-6

kernel_name = 'hybrid_ssd_pool_macaron_decode_step'


def rmsnorm(x, g):
    xf = x.astype(jnp.float32)
    out = xf * lax.rsqrt(jnp.mean(xf * xf, axis=-1, keepdims=True) + EPS) * g.astype(jnp.float32)
    return out.astype(x.dtype)


def swiglu(h, w_in, w_out):
    gate, up = jnp.split(h @ w_in, 2, axis=-1)
    return (jax.nn.silu(gate) * up) @ w_out


def depthwise_causal_conv(full, w, b):
    c = full.shape[-1]
    out = lax.conv_general_dilated(full, w[:, None, :].astype(full.dtype), window_strides=(1,), padding='VALID',
                                   dimension_numbers=('NWC', 'WIO', 'NWC'), feature_group_count=c)
    return out + b.astype(full.dtype)


def ssd_scan(xh, dt, a_neg, bm, cm, init, front, chunk):
    b, t, h, p = xh.shape
    g, n = bm.shape[2], bm.shape[3]
    r = h // g
    back = (-(front + t)) % chunk

    def pad(v):
        return jnp.pad(v, [(0, 0), (front, back)] + [(0, 0)] * (v.ndim - 2))

    xh, dt, bm, cm = pad(xh), pad(dt), pad(bm), pad(cm)
    nc = (front + t + back) // chunk
    xc = xh.reshape(b, nc, chunk, g, r, p)
    dtc = dt.reshape(b, nc, chunk, g, r)
    bc = bm.reshape(b, nc, chunk, g, n)
    cc = cm.reshape(b, nc, chunk, g, n)
    acum = jnp.cumsum(dtc * a_neg.reshape(g, r), axis=2)
    xdt = xc * dtc[..., None]
    causal = jnp.tril(jnp.ones((chunk, chunk), dtype=bool))[:, :, None, None]
    seg = acum[:, :, :, None] - acum[:, :, None, :]
    lmat = jnp.exp(jnp.where(causal, seg, -jnp.inf))
    cb = jnp.einsum('bcign,bcjgn->bcijg', cc, bc)
    y_intra = jnp.einsum('bcijgr,bcjgrp->bcigrp', cb[..., None] * lmat, xdt)
    decay_end = jnp.exp(acum[:, :, -1:] - acum)
    cs = jnp.einsum('bcjgn,bcjgrp->bcgrpn', bc, xdt * decay_end[..., None])
    chunk_decay = jnp.exp(acum[:, :, -1])

    def step(s, inp):
        cs_c, d_c = inp
        return s * d_c[..., None, None] + cs_c, s

    final, prev = lax.scan(step, init.reshape(b, g, r, p, n),
                           (jnp.moveaxis(cs, 1, 0), jnp.moveaxis(chunk_decay, 1, 0)))
    prev = jnp.moveaxis(prev, 0, 1)
    y_inter = jnp.einsum('bcign,bcgrpn->bcigrp', cc, prev) * jnp.exp(acum)[..., None]
    y = (y_intra + y_inter).reshape(b, nc * chunk, h, p)[:, front:front + t]
    return y, final.reshape(b, h, p, n)


def multiscale_pool(u, buf, pos0, w_grp, scale):
    b, t, _ = u.shape
    pw = MAX_WIN - 1
    full = jnp.concatenate([buf.astype(u.dtype), u], axis=1)
    csum = jnp.cumsum(jnp.pad(full.astype(jnp.float32), ((0, 0), (1, 0), (0, 0))), axis=1)
    pos = pos0 + jnp.arange(t)
    means = []
    for k, w in enumerate(POOL_WINDOWS):
        lo, hi = k * POOL_GROUP_DIM, (k + 1) * POOL_GROUP_DIM
        s = csum[:, pw + 1:pw + 1 + t, lo:hi] - csum[:, pw + 1 - w:pw + 1 - w + t, lo:hi]
        cnt = jnp.minimum(pos + 1, w).astype(jnp.float32)
        means.append(s / cnt[None, :, None])
    pooled = jnp.stack(means, axis=2) - u.astype(jnp.float32).reshape(b, t, N_POOL_GROUPS, POOL_GROUP_DIM)
    mixed = jnp.einsum('btgc,gcd->btgd', pooled.astype(u.dtype), w_grp).reshape(b, t, POOL_DIM) * scale
    return mixed, full[:, -pw:]


def mixer(h, ssm0, conv0, pool0, pos0, front, chunk, prm, l):
    b, t, _ = h.shape
    f32 = jnp.float32
    proj = h @ prm['w_in'][l]
    o1 = D_INNER
    o2 = o1 + CONV_DIM
    o3 = o2 + N_HEADS
    o4 = o3 + POOL_DIM
    o5 = o4 + D_MODEL
    z, xbc, dt_raw = proj[..., :o1], proj[..., o1:o2], proj[..., o2:o3]
    u, g_ssd, g_pool = proj[..., o3:o4], proj[..., o4:o5], proj[..., o5:]
    full = jnp.concatenate([conv0.astype(xbc.dtype), xbc], axis=1)
    xbc = jax.nn.silu(depthwise_causal_conv(full, prm['conv_w'][l], prm['conv_b'][l]))
    new_conv = full[:, -(CONV_K - 1):]
    gn = N_GROUPS * D_STATE
    xs = xbc[..., :D_INNER].reshape(b, t, N_HEADS, HEAD_DIM).astype(f32)
    bm = xbc[..., D_INNER:D_INNER + gn].reshape(b, t, N_GROUPS, D_STATE).astype(f32)
    cm = xbc[..., D_INNER + gn:].reshape(b, t, N_GROUPS, D_STATE).astype(f32)
    dt = jax.nn.softplus(dt_raw.astype(f32) + prm['dt_bias'][l].astype(f32))
    a_neg = -jnp.exp(prm['a_log'][l].astype(f32))
    y, new_ssm = ssd_scan(xs, dt, a_neg, bm, cm, ssm0.astype(f32), front, chunk)
    y = y + prm['d_skip'][l].astype(f32)[:, None] * xs
    y = y.reshape(b, t, D_INNER).astype(h.dtype)
    y = rmsnorm(y * jax.nn.silu(z), prm['ssd_norm'][l])
    ssd_out = y @ prm['w_ssd_branch'][l]
    pooled, new_pool = multiscale_pool(u, pool0, pos0, prm['pool_w'][l], prm['pool_scale'][l])
    pool_out = pooled @ prm['w_pool_branch'][l]
    merged = jax.nn.sigmoid(g_ssd) * ssd_out + jax.nn.sigmoid(g_pool) * pool_out
    return merged @ prm['w_out'][l], new_ssm.astype(h.dtype), new_conv, new_pool


def run_trunk(x, ssm, conv, pool, pos0, front, chunk, prm):
    new_ssm, new_conv, new_pool = [], [], []
    for l in range(DEPTH):
        f = swiglu(rmsnorm(x, prm['ln_ffn1_pre'][l]), prm['ffn1_w_in'][l], prm['ffn1_w_out'][l])
        x = x + 0.5 * rmsnorm(f, prm['ln_ffn1_post'][l])
        m, s, c, p = mixer(rmsnorm(x, prm['ln_mix_pre'][l]), ssm[l], conv[l], pool[l], pos0, front, chunk, prm, l)
        x = x + rmsnorm(m, prm['ln_mix_post'][l])
        f = swiglu(rmsnorm(x, prm['ln_ffn2_pre'][l]), prm['ffn2_w_in'][l], prm['ffn2_w_out'][l])
        x = x + 0.5 * rmsnorm(f, prm['ln_ffn2_post'][l])
        new_ssm.append(s)
        new_conv.append(c)
        new_pool.append(p)
    return x, jnp.stack(new_ssm), jnp.stack(new_conv), jnp.stack(new_pool)


def setup_inputs(seed: int = 0) -> dict:
    key = jax.random.key(seed)
    ks = iter(jax.random.split(key, 40))
    f32 = jnp.float32

    def nrm(shape, scale):
        return jax.random.normal(next(ks), shape, f32) * scale

    def gain(width):
        return 1.0 + nrm((DEPTH, width), 0.05)

    x_prompt = nrm((BATCH, SEQ, D_MODEL), 1.0)
    x_sample = nrm((DEC_BATCH, DEC_SEQ, D_MODEL), 1.0)
    state_ssm = nrm((DEPTH, DEC_BATCH, N_HEADS, HEAD_DIM, D_STATE), 0.1)
    state_conv = nrm((DEPTH, DEC_BATCH, CONV_K - 1, CONV_DIM), 1.0)
    state_pool = nrm((DEPTH, DEC_BATCH, MAX_WIN - 1, POOL_DIM), 1.0)
    meta_tokens = nrm((N_META, D_MODEL), 1.0)
    ln_ffn1_pre = gain(D_MODEL)
    ln_ffn1_post = gain(D_MODEL)
    ffn1_w_in = nrm((DEPTH, D_MODEL, 2 * D_FF), D_MODEL ** -0.5)
    ffn1_w_out = nrm((DEPTH, D_FF, D_MODEL), D_FF ** -0.5)
    ln_mix_pre = gain(D_MODEL)
    ln_mix_post = gain(D_MODEL)
    w_in = nrm((DEPTH, D_MODEL, IN_COLS), D_MODEL ** -0.5)
    conv_w = nrm((DEPTH, CONV_K, CONV_DIM), CONV_K ** -0.5)
    conv_b = nrm((DEPTH, CONV_DIM), 0.02)
    dt0 = jnp.exp(jax.random.uniform(next(ks), (DEPTH, N_HEADS), f32, math.log(1e-3), math.log(1e-1)))
    dt_bias = dt0 + jnp.log(-jnp.expm1(-dt0))
    a_log = jnp.log(jax.random.uniform(next(ks), (DEPTH, N_HEADS), f32, 1.0, 16.0))
    d_skip = 1.0 + nrm((DEPTH, N_HEADS), 0.1)
    ssd_norm = gain(D_INNER)
    w_ssd_branch = nrm((DEPTH, D_INNER, D_MODEL), D_INNER ** -0.5)
    pool_w = nrm((DEPTH, N_POOL_GROUPS, POOL_GROUP_DIM, POOL_GROUP_DIM), POOL_GROUP_DIM ** -0.5)
    pool_scale = gain(POOL_DIM)
    w_pool_branch = nrm((DEPTH, POOL_DIM, D_MODEL), POOL_DIM ** -0.5)
    w_out = nrm((DEPTH, D_MODEL, D_MODEL), D_MODEL ** -0.5)
    ln_ffn2_pre = gain(D_MODEL)
    ln_ffn2_post = gain(D_MODEL)
    ffn2_w_in = nrm((DEPTH, D_MODEL, 2 * D_FF), D_MODEL ** -0.5)
    ffn2_w_out = nrm((DEPTH, D_FF, D_MODEL), D_FF ** -0.5)
    return {'x_prompt': x_prompt, 'x_sample': x_sample, 'state_ssm': state_ssm, 'state_conv': state_conv,
            'state_pool': state_pool, 'meta_tokens': meta_tokens, 'ln_ffn1_pre': ln_ffn1_pre,
            'ln_ffn1_post': ln_ffn1_post, 'ffn1_w_in': ffn1_w_in, 'ffn1_w_out': ffn1_w_out,
            'ln_mix_pre': ln_mix_pre, 'ln_mix_post': ln_mix_post, 'w_in': w_in, 'conv_w': conv_w,
            'conv_b': conv_b, 'dt_bias': dt_bias, 'a_log': a_log, 'd_skip': d_skip, 'ssd_norm': ssd_norm,
            'w_ssd_branch': w_ssd_branch, 'pool_w': pool_w, 'pool_scale': pool_scale,
            'w_pool_branch': w_pool_branch, 'w_out': w_out, 'ln_ffn2_pre': ln_ffn2_pre,
            'ln_ffn2_post': ln_ffn2_post, 'ffn2_w_in': ffn2_w_in, 'ffn2_w_out': ffn2_w_out}


def reference(x_prompt, x_sample, state_ssm, state_conv, state_pool, meta_tokens, ln_ffn1_pre, ln_ffn1_post,
              ffn1_w_in, ffn1_w_out, ln_mix_pre, ln_mix_post, w_in, conv_w, conv_b, dt_bias, a_log, d_skip,
              ssd_norm, w_ssd_branch, pool_w, pool_scale, w_pool_branch, w_out, ln_ffn2_pre, ln_ffn2_post,
              ffn2_w_in, ffn2_w_out):
    prm = dict(ln_ffn1_pre=ln_ffn1_pre, ln_ffn1_post=ln_ffn1_post, ffn1_w_in=ffn1_w_in, ffn1_w_out=ffn1_w_out,
               ln_mix_pre=ln_mix_pre, ln_mix_post=ln_mix_post, w_in=w_in, conv_w=conv_w, conv_b=conv_b,
               dt_bias=dt_bias, a_log=a_log, d_skip=d_skip, ssd_norm=ssd_norm, w_ssd_branch=w_ssd_branch,
               pool_w=pool_w, pool_scale=pool_scale, w_pool_branch=w_pool_branch, w_out=w_out,
               ln_ffn2_pre=ln_ffn2_pre, ln_ffn2_post=ln_ffn2_post, ffn2_w_in=ffn2_w_in, ffn2_w_out=ffn2_w_out)
    bp = x_prompt.shape[0]
    dtype = x_prompt.dtype
    meta = jnp.broadcast_to(meta_tokens.astype(dtype)[None], (bp, N_META, D_MODEL))
    xp = jnp.concatenate([meta, x_prompt], axis=1)
    ssm_z = jnp.zeros((DEPTH, bp, N_HEADS, HEAD_DIM, D_STATE), dtype)
    conv_z = jnp.zeros((DEPTH, bp, CONV_K - 1, CONV_DIM), dtype)
    pool_z = jnp.zeros((DEPTH, bp, MAX_WIN - 1, POOL_DIM), dtype)
    front = (-N_META) % CHUNK
    yp, ssm_p, conv_p, pool_p = run_trunk(xp, ssm_z, conv_z, pool_z, 0, front, CHUNK, prm)
    y_prompt = yp[:, N_META:]
    sample_chunk = min(CHUNK, x_sample.shape[1])
    y_sample, ssm_s, conv_s, pool_s = run_trunk(x_sample, state_ssm, state_conv, state_pool, PAST_LEN, 0,
                                                sample_chunk, prm)
    return (y_prompt, y_sample, ssm_p, conv_p, pool_p, ssm_s, conv_s, pool_s)
```

```python
import functools

import jax
import jax.numpy as jnp
from jax import lax
from jax.experimental import pallas as pl
from jax.experimental.pallas import tpu as pltpu

F32 = jnp.float32
BF16 = jnp.bfloat16

D_MODEL = 1024
DEPTH = 2
N_META = 16
D_INNER = 2048
HEAD_DIM = 64
N_HEADS = 32
N_GROUPS = 4
HEADS_PER_GROUP = N_HEADS // N_GROUPS
D_STATE = 128
CONV_K = 4
CONV_DIM = D_INNER + 2 * N_GROUPS * D_STATE
POOL_WINDOWS = (2, 4, 8, 16)
POOL_DIM = 1024
POOL_GROUP_DIM = 256
MAX_WIN = 16
D_FF = 2816
EPS = 1e-6
PAST_LEN = 16384

LANES = 128
VMEM_LIMIT = 56 * 1024 * 1024

ROW_TILE = 512
SEQ_BLOCK = 128
SAMPLE_SEQS = 8
GROUP_COLS = HEADS_PER_GROUP * HEAD_DIM


def _rms(x, g):
    return x * lax.rsqrt(jnp.mean(x * x, axis=-1, keepdims=True) + EPS) * g


def _dot(a, b):
    return jnp.dot(a, b, preferred_element_type=F32)


def _dot_nt(a, b):
    return lax.dot_general(a, b, (((1,), (1,)), ((), ())), preferred_element_type=F32)


def _dot_tn(a, b):
    return lax.dot_general(a, b, (((0,), (0,)), ((), ())), preferred_element_type=F32)


def _dot_exact(a, b):
    return jnp.dot(a, b, precision=lax.Precision.HIGHEST, preferred_element_type=F32)


def _resident(shape):
    nd = len(shape)
    return pl.BlockSpec(shape, lambda *_: (0,) * nd, pipeline_mode=pl.Buffered(1))


def _params(n_axes):
    return pltpu.CompilerParams(dimension_semantics=("arbitrary",) * n_axes,
                                vmem_limit_bytes=VMEM_LIMIT)


def _ffn_body(x_ref, gpre_ref, win_ref, wout_ref, gpost_ref, o_ref):
    x = x_ref[...]
    h = _rms(x, gpre_ref[...]).astype(BF16)
    gate = _dot(h, win_ref[:, :D_FF])
    up = _dot(h, win_ref[:, D_FF:])
    a = (jax.nn.silu(gate) * up).astype(BF16)
    f = _dot(a, wout_ref[...])
    o_ref[...] = x + 0.5 * _rms(f, gpost_ref[...])


def _ffn(x, gpre, w_in, w_out, gpost):
    n = x.shape[0]
    row = pl.BlockSpec((ROW_TILE, D_MODEL), lambda i: (i, 0))
    return pl.pallas_call(
        _ffn_body,
        grid=(n // ROW_TILE,),
        in_specs=[row, _resident((1, D_MODEL)), _resident((D_MODEL, 2 * D_FF)),
                  _resident((D_FF, D_MODEL)), _resident((1, D_MODEL))],
        out_specs=row,
        out_shape=jax.ShapeDtypeStruct((n, D_MODEL), F32),
        compiler_params=_params(1),
        name="ffn",
    )(x, gpre, w_in, w_out, gpost)


_C_XBC = CONV_DIM
_C_U = _C_XBC + POOL_DIM
_C_Z = _C_U + D_INNER
_C_G = _C_Z + 2 * D_MODEL


def _inproj_body(x_ref, g_ref, w_ref, wdt_ref, xbc_ref, u_ref, z_ref, gate_ref, dt_ref):
    h = _rms(x_ref[...], g_ref[...]).astype(BF16)
    xbc_ref[...] = _dot(h, w_ref[:, :_C_XBC]).astype(BF16)
    u_ref[...] = _dot(h, w_ref[:, _C_XBC:_C_U]).astype(BF16)
    z_ref[...] = _dot(h, w_ref[:, _C_U:_C_Z]).astype(BF16)
    gate_ref[...] = _dot(h, w_ref[:, _C_Z:_C_G]).astype(BF16)
    dt_ref[...] = _dot(h, wdt_ref[...])


def _inproj(x, g, w_main, w_dt):
    n = x.shape[0]

    def row(c):
        return pl.BlockSpec((ROW_TILE, c), lambda i: (i, 0))

    widths = (CONV_DIM, POOL_DIM, D_INNER, 2 * D_MODEL)
    return pl.pallas_call(
        _inproj_body,
        grid=(n // ROW_TILE,),
        in_specs=[row(D_MODEL), _resident((1, D_MODEL)), _resident((D_MODEL, _C_G)),
                  _resident((D_MODEL, LANES))],
        out_specs=[row(c) for c in widths] + [row(LANES)],
        out_shape=[jax.ShapeDtypeStruct((n, c), BF16) for c in widths]
        + [jax.ShapeDtypeStruct((n, LANES), F32)],
        compiler_params=_params(1),
        name="inproj",
    )(x, g, w_main, w_dt)


def _mixout_body(x_ref, yg_ref, pooled_ref, gate_ref, wssd_ref, poolw_ref, pscale_ref, wpb_ref,
                 wout_ref, gpost_ref, o_ref):
    ssd_out = _dot(yg_ref[...], wssd_ref[...])
    pooled = pooled_ref[...]
    mixed = jnp.concatenate(
        [_dot(pooled[:, k * POOL_GROUP_DIM:(k + 1) * POOL_GROUP_DIM], poolw_ref[k])
         for k in range(len(POOL_WINDOWS))], axis=1) * pscale_ref[...]
    pool_out = _dot(mixed.astype(BF16), wpb_ref[...])
    gates = gate_ref[...].astype(F32)
    merged = (jax.nn.sigmoid(gates[:, :D_MODEL]) * ssd_out
              + jax.nn.sigmoid(gates[:, D_MODEL:]) * pool_out)
    m = _dot(merged.astype(BF16), wout_ref[...])
    o_ref[...] = x_ref[...] + _rms(m, gpost_ref[...])


def _mixout(x, yg, pooled, gates, w_ssd, pool_w, pool_scale, w_pb, w_out, gpost):
    n = x.shape[0]

    def row(c):
        return pl.BlockSpec((ROW_TILE, c), lambda i: (i, 0))

    return pl.pallas_call(
        _mixout_body,
        grid=(n // ROW_TILE,),
        in_specs=[row(D_MODEL), row(D_INNER), row(POOL_DIM), row(2 * D_MODEL),
                  _resident((D_INNER, D_MODEL)),
                  _resident((len(POOL_WINDOWS), POOL_GROUP_DIM, POOL_GROUP_DIM)),
                  _resident((1, POOL_DIM)), _resident((POOL_DIM, D_MODEL)),
                  _resident((D_MODEL, D_MODEL)), _resident((1, D_MODEL))],
        out_specs=row(D_MODEL),
        out_shape=jax.ShapeDtypeStruct((n, D_MODEL), F32),
        compiler_params=_params(1),
        name="mixout",
    )(x, yg, pooled, gates, w_ssd, pool_w, pool_scale, w_pb, w_out, gpost)


def _expand_heads(q, e3):
    b1 = q.astype(BF16).astype(F32)
    r1 = q - b1
    b2 = r1.astype(BF16).astype(F32)
    b3 = (r1 - b2).astype(BF16).astype(F32)
    lane = lax.broadcasted_iota(jnp.int32, (1, LANES), 1)
    packed = jnp.where(lane < N_HEADS, b1,
                       jnp.where(lane < 2 * N_HEADS, pltpu.roll(b2, N_HEADS, axis=1),
                                 jnp.where(lane < 3 * N_HEADS, pltpu.roll(b3, 2 * N_HEADS, axis=1),
                                           0.0)))
    return _dot(packed.astype(BF16), e3)


def _seq_body(xbc_ref, u_ref, z_ref, dt_ref, convw_ref, convb_ref, dtb_ref, alog_ref, dskip_ref,
              snorm_ref, e3_ref, ssm0_ref, conv0_ref, pool0_ref, *rest,
              rows, seqs, n_real, front, pos0, n_alias):
    yg_ref, pooled_ref, ssm_ref, conv_ref, pool_ref, cext, pext = rest[n_alias:]
    ts = rows // seqs
    j = pl.program_id(1)

    @pl.when(j == 0)
    def _():
        ssm_ref[...] = ssm0_ref[...]
        cext[:, CONV_K + 1:8, :] = conv0_ref[...]
        pext[:, 0:1, :] = jnp.zeros((seqs, 1, POOL_DIM), F32)
        pext[:, 1:MAX_WIN, :] = pool0_ref[...]

    def step():
        row_id = lax.broadcasted_iota(jnp.int32, (rows, 1), 0)
        xbc = xbc_ref[...].astype(F32)
        u = u_ref[...].astype(F32)
        dt_raw = dt_ref[...]
        if front:
            valid = row_id >= front
            xbc = jnp.where(valid, xbc, 0.0)
            u = jnp.where(valid, u, 0.0)

        conv_parts = []
        for s in range(seqs):
            cext[s, 8:8 + ts, :] = xbc[s * ts:(s + 1) * ts, :]
            acc = convb_ref[...]
            for k in range(CONV_K):
                acc = acc + convw_ref[k:k + 1, :] * cext[s, 8 - (CONV_K - 1) + k:8 - (CONV_K - 1) + k + ts, :]
            conv_parts.append(acc)
            tail = cext[s, 8 + ts - (CONV_K - 1):8 + ts, :]
            cext[s, 8 - (CONV_K - 1):8, :] = tail
            conv_ref[s] = tail
        xact = jax.nn.silu(jnp.concatenate(conv_parts, axis=0) if seqs > 1 else conv_parts[0])
        xs = xact[:, :D_INNER]
        bm = xact[:, D_INNER:D_INNER + N_GROUPS * D_STATE]
        cm = xact[:, D_INNER + N_GROUPS * D_STATE:]
        bb = bm.astype(BF16)
        cb16 = cm.astype(BF16)

        xdt = dt_raw + dtb_ref[...]
        dt = jnp.maximum(xdt, 0.0) + jnp.log1p(jnp.exp(-jnp.abs(xdt)))
        if front:
            dt = jnp.where(valid, dt, 0.0)
        dta = dt * (-jnp.exp(alog_ref[...]))
        r = lax.broadcasted_iota(jnp.int32, (rows, rows), 0)
        c = lax.broadcasted_iota(jnp.int32, (rows, rows), 1)
        if seqs > 1:
            shift = ts.bit_length() - 1
            same = (r >> shift) == (c >> shift)
            causal = (r >= c) & same
            tot = _dot_exact(same.astype(F32), dta)
        else:
            causal = r >= c
            tot = _dot_exact(jnp.ones((rows, rows), F32), dta)
        acum = _dot_exact(causal.astype(F32), dta)
        acum_t = acum.T
        q = jnp.concatenate([dt, jnp.exp(acum), dt * jnp.exp(tot - acum)], axis=0)
        qe = _expand_heads(q, e3_ref[...])
        dt_e, eacum_e, w_e = qe[:rows], qe[rows:2 * rows], qe[2 * rows:]
        xdt16 = (xs * dt_e).astype(BF16)
        xw16 = (xs * w_e).astype(BF16)

        low = lax.broadcasted_iota(jnp.int32, (1, LANES), 1) < HEAD_DIM
        tiles = []
        for g in range(N_GROUPS):
            cbm = _dot_nt(cb16[:, g * D_STATE:(g + 1) * D_STATE], bb[:, g * D_STATE:(g + 1) * D_STATE])
            for hh in range(0, HEADS_PER_GROUP, 2):
                h0 = g * HEADS_PER_GROUP + hh
                t = h0 // 2
                xt = xdt16[:, t * LANES:(t + 1) * LANES]
                zero = jnp.zeros_like(xt)
                rhs = jnp.concatenate([jnp.where(low, xt, zero), jnp.where(low, zero, xt)], axis=0)
                ms = []
                for h in (h0, h0 + 1):
                    seg = acum[:, h:h + 1] - acum_t[h:h + 1, :]
                    ms.append((cbm * jnp.exp(jnp.where(causal, seg, -jnp.inf))).astype(BF16))
                tiles.append(_dot(jnp.concatenate(ms, axis=1), rhs))
        y = jnp.concatenate(tiles, axis=1)

        inter_parts = []
        for s in range(seqs):
            rs = slice(s * ts, (s + 1) * ts)
            state = ssm_ref[s]
            state16 = state.astype(BF16)
            c_s = cm[rs].astype(BF16)
            b_s = bm[rs].astype(BF16)
            xw_s = (xs[rs] * w_e[rs]).astype(BF16) if seqs > 1 else xw16
            inter_parts.append(jnp.concatenate(
                [_dot_nt(c_s[:, g * D_STATE:(g + 1) * D_STATE], state16[g * GROUP_COLS:(g + 1) * GROUP_COLS, :])
                 for g in range(N_GROUPS)], axis=1))
            cdec = jnp.exp(tot[s * ts:s * ts + 1, :])
            new_rows = []
            for g in range(N_GROUPS):
                inc = _dot_tn(xw_s[:, g * GROUP_COLS:(g + 1) * GROUP_COLS], b_s[:, g * D_STATE:(g + 1) * D_STATE])
                for hh in range(HEADS_PER_GROUP):
                    h = g * HEADS_PER_GROUP + hh
                    new_rows.append(state[h * HEAD_DIM:(h + 1) * HEAD_DIM, :] * cdec[:, h:h + 1]
                                    + inc[hh * HEAD_DIM:(hh + 1) * HEAD_DIM, :])
            ssm_ref[s] = jnp.concatenate(new_rows, axis=0)
        y_inter = jnp.concatenate(inter_parts, axis=0) if seqs > 1 else inter_parts[0]
        y = y + y_inter * eacum_e + dskip_ref[...] * xs
        y = y * jax.nn.silu(z_ref[...].astype(F32))
        yg_ref[...] = _rms(y, snorm_ref[...]).astype(BF16)

        if front:
            pos = jnp.maximum(row_id - front, 0) + pos0
        pooled_parts = []
        for s in range(seqs):
            pext[s, MAX_WIN:MAX_WIN + ts, :] = u[s * ts:(s + 1) * ts, :]
            full = pext[s]
            outs = []
            for k, w in enumerate(POOL_WINDOWS):
                fk = full[:, k * POOL_GROUP_DIM:(k + 1) * POOL_GROUP_DIM]
                acc = fk
                sh = 1
                while sh < w:
                    acc = acc + pltpu.roll(acc, sh, axis=0)
                    sh *= 2
                wsum = acc[MAX_WIN:MAX_WIN + ts]
                if front:
                    cnt = jnp.minimum(pos + 1, w).astype(F32)
                    mean = wsum / cnt
                else:
                    mean = wsum * (1.0 / w)
                outs.append(mean - fk[MAX_WIN:MAX_WIN + ts])
            pooled_parts.append(jnp.concatenate(outs, axis=1))
            new_tail = full[ts:ts + MAX_WIN]
            pext[s, 0:MAX_WIN, :] = new_tail
            pool_ref[s] = new_tail[1:]
        pooled = jnp.concatenate(pooled_parts, axis=0) if seqs > 1 else pooled_parts[0]
        pooled_ref[...] = pooled.astype(BF16)

    if n_real is None:
        step()
    else:
        pl.when(j < n_real)(step)

        @pl.when(j >= n_real)
        def _():
            yg_ref[...] = jnp.zeros_like(yg_ref)
            pooled_ref[...] = jnp.zeros_like(pooled_ref)


def _seq_call(proj, prm, l, states, carry, *, n_seq_blocks, n_time_blocks, rows, seqs, row_block0,
              n_real, front, pos0, name):
    xbc, u, z, dt = proj
    nt = xbc.shape[0]
    (ssm0, ssm0_spec), (conv0, conv0_spec), (pool0, pool0_spec) = states

    def row(c):
        return pl.BlockSpec((rows, c), lambda b, j: (row_block0 + b * n_time_blocks + j, 0))

    def const2(shape):
        return pl.BlockSpec(shape, lambda b, j: (0, 0))

    n_b = n_seq_blocks * seqs
    in_specs = [row(CONV_DIM), row(POOL_DIM), row(D_INNER), row(LANES),
                const2((CONV_K, CONV_DIM)), const2((1, CONV_DIM)), const2((1, LANES)),
                const2((1, LANES)), const2((1, D_INNER)), const2((1, D_INNER)),
                const2((LANES, D_INNER)), ssm0_spec, conv0_spec, pool0_spec]
    args = [xbc, u, z, dt, prm["conv_w"][l], prm["conv_b"][l], prm["dt_bias"][l], prm["a_log"][l],
            prm["d_skip"][l], prm["ssd_norm"][l], prm["e3"], ssm0, conv0, pool0]
    out_shape = [jax.ShapeDtypeStruct((nt, D_INNER), BF16), jax.ShapeDtypeStruct((nt, POOL_DIM), BF16)]
    out_specs = [row(D_INNER), row(POOL_DIM)]
    aliases = {}
    n_alias = 0
    if carry is not None:
        n_alias = len(carry)
        for k, buf in enumerate(carry):
            aliases[len(args)] = k
            args.append(buf)
            in_specs.append(pl.BlockSpec(memory_space=pl.ANY))
    state_out_shapes, state_out_specs = [], []
    return_slots = []
    for arr_shape, spec, alias_buf in prm["state_outs"]:
        state_out_shapes.append(arr_shape)
        state_out_specs.append(spec)
        if alias_buf is not None:
            aliases[len(args)] = len(out_shape) + len(state_out_shapes) - 1
            args.append(alias_buf)
            in_specs.append(pl.BlockSpec(memory_space=pl.ANY))
            n_alias += 1
    body = functools.partial(_seq_body, rows=rows, seqs=seqs, n_real=n_real, front=front, pos0=pos0,
                             n_alias=n_alias)
    ts = rows // seqs
    return pl.pallas_call(
        body,
        grid=(n_seq_blocks, n_time_blocks),
        in_specs=in_specs,
        out_specs=out_specs + state_out_specs,
        out_shape=out_shape + state_out_shapes,
        scratch_shapes=[pltpu.VMEM((seqs, ts + 8, CONV_DIM), F32),
                        pltpu.VMEM((seqs, ts + MAX_WIN, POOL_DIM), F32)],
        input_output_aliases=aliases,
        compiler_params=_params(2),
        name=name,
    )(*args)


def _state_specs(n_seqs_per_block, index):
    s = n_seqs_per_block
    return (pl.BlockSpec((s, D_INNER, D_STATE), lambda b, j: (index(b), 0, 0)),
            pl.BlockSpec((s, CONV_K - 1, CONV_DIM), lambda b, j: (index(b), 0, 0)),
            pl.BlockSpec((s, MAX_WIN - 1, POOL_DIM), lambda b, j: (index(b), 0, 0)))


def kernel(x_prompt, x_sample, state_ssm, state_conv, state_pool, meta_tokens, ln_ffn1_pre, ln_ffn1_post,
           ffn1_w_in, ffn1_w_out, ln_mix_pre, ln_mix_post, w_in, conv_w, conv_b, dt_bias, a_log, d_skip,
           ssd_norm, w_ssd_branch, pool_w, pool_scale, w_pool_branch, w_out, ln_ffn2_pre, ln_ffn2_post,
           ffn2_w_in, ffn2_w_out):
    n_prompt, seq_len, _ = x_prompt.shape
    n_sample, dec_len, _ = x_sample.shape
    prompt_rows = n_prompt * seq_len
    sample_rows = n_sample * dec_len
    meta_front = SEQ_BLOCK - N_META
    used_rows = prompt_rows + sample_rows + SEQ_BLOCK
    total_rows = -(-used_rows // ROW_TILE) * ROW_TILE
    sample_block_rows = SAMPLE_SEQS * dec_len
    assert seq_len % SEQ_BLOCK == 0 and prompt_rows % sample_block_rows == 0
    assert n_sample % SAMPLE_SEQS == 0 and (total_rows - prompt_rows - sample_rows) % SEQ_BLOCK == 0

    x = jnp.concatenate([
        x_prompt.reshape(prompt_rows, D_MODEL), x_sample.reshape(sample_rows, D_MODEL),
        jnp.zeros((meta_front, D_MODEL), F32), meta_tokens.astype(F32),
        jnp.zeros((total_rows - used_rows, D_MODEL), F32)], axis=0)

    o1 = D_INNER
    o2 = o1 + CONV_DIM
    o3 = o2 + N_HEADS
    o4 = o3 + POOL_DIM
    w_main = jnp.concatenate([w_in[..., o1:o2], w_in[..., o3:o4], w_in[..., :o1], w_in[..., o4:]],
                             axis=-1).astype(BF16)
    w_dt = jnp.pad(w_in[..., o2:o3], ((0, 0), (0, 0), (0, LANES - N_HEADS))).astype(BF16)
    head_of_lane = jnp.arange(LANES) % N_HEADS
    e3 = ((head_of_lane[:, None] == (jnp.arange(D_INNER) // HEAD_DIM)[None, :])
          & (jnp.arange(LANES) < 3 * N_HEADS)[:, None]).astype(BF16)

    def vec(p, width=None):
        p = p.astype(F32)
        if width is not None:
            p = jnp.pad(p, ((0, 0), (0, width - p.shape[-1])))
        return p[:, None, :]

    prm = {
        "conv_w": conv_w.astype(F32), "conv_b": vec(conv_b), "dt_bias": vec(dt_bias, LANES),
        "a_log": vec(a_log, LANES), "d_skip": vec(jnp.repeat(d_skip, HEAD_DIM, axis=-1)),
        "ssd_norm": vec(ssd_norm), "e3": e3,
    }
    ffn1_in16, ffn1_out16 = ffn1_w_in.astype(BF16), ffn1_w_out.astype(BF16)
    ffn2_in16, ffn2_out16 = ffn2_w_in.astype(BF16), ffn2_w_out.astype(BF16)
    w_ssd16, pool_w16 = w_ssd_branch.astype(BF16), pool_w.astype(BF16)
    w_pb16, w_out16 = w_pool_branch.astype(BF16), w_out.astype(BF16)

    ssm_s_in = state_ssm.reshape(DEPTH, n_sample, D_INNER, D_STATE).astype(F32)
    conv_s_in = state_conv.astype(F32)
    pool_s_in = state_pool.astype(F32)
    zero_states = (jnp.zeros((1, D_INNER, D_STATE), F32), jnp.zeros((1, CONV_K - 1, CONV_DIM), F32),
                   jnp.zeros((1, MAX_WIN - 1, POOL_DIM), F32))

    n_time = seq_len // SEQ_BLOCK
    meta_block0 = (prompt_rows + sample_rows) // SEQ_BLOCK
    n_meta_blocks = (total_rows - prompt_rows - sample_rows) // SEQ_BLOCK
    sample_block0 = prompt_rows // sample_block_rows

    def state_out(n, specs, ssm_alias=None, ssm_shape=None):
        return [(jax.ShapeDtypeStruct(ssm_shape or (n, D_INNER, D_STATE), F32), specs[0], ssm_alias),
                (jax.ShapeDtypeStruct((n, CONV_K - 1, CONV_DIM), F32), specs[1], None),
                (jax.ShapeDtypeStruct((n, MAX_WIN - 1, POOL_DIM), F32), specs[2], None)]

    ssm_p, conv_p, pool_p, conv_s, pool_s = [], [], [], [], []
    ssm_s = None
    for l in range(DEPTH):
        x = _ffn(x, vec(ln_ffn1_pre)[l], ffn1_in16[l], ffn1_out16[l], vec(ln_ffn1_post)[l])
        proj = _inproj(x, vec(ln_mix_pre)[l], w_main[l], w_dt[l])
        xbc, u, z, gates, dt = proj
        seq_in = (xbc, u, z, dt)

        one = _state_specs(1, lambda b: 0)
        prm["state_outs"] = state_out(1, one)
        yg, pooled, m_ssm, m_conv, m_pool = _seq_call(
            seq_in, prm, l, list(zip(zero_states, one)), None,
            n_seq_blocks=1, n_time_blocks=n_meta_blocks, rows=SEQ_BLOCK, seqs=1,
            row_block0=meta_block0, n_real=1, front=meta_front, pos0=0, name="seq_meta")

        per_seq = _state_specs(1, lambda b: b)
        prm["state_outs"] = state_out(n_prompt, per_seq)
        yg, pooled, p_ssm, p_conv, p_pool = _seq_call(
            seq_in, prm, l, list(zip((m_ssm, m_conv, m_pool), one)), (yg, pooled),
            n_seq_blocks=n_prompt, n_time_blocks=n_time, rows=SEQ_BLOCK, seqs=1,
            row_block0=0, n_real=None, front=0, pos0=N_META, name="seq_prompt")
        ssm_p.append(p_ssm)
        conv_p.append(p_conv)
        pool_p.append(p_pool)

        blk = _state_specs(SAMPLE_SEQS, lambda b: b)
        layer_ssm_spec = pl.BlockSpec((None, SAMPLE_SEQS, D_INNER, D_STATE), lambda b, j, l=l: (l, b, 0, 0))
        prm["state_outs"] = state_out(n_sample, (layer_ssm_spec, blk[1], blk[2]), ssm_alias=ssm_s,
                                      ssm_shape=(DEPTH, n_sample, D_INNER, D_STATE))
        yg, pooled, ssm_s, s_conv, s_pool = _seq_call(
            seq_in, prm, l,
            [(ssm_s_in, layer_ssm_spec), (conv_s_in[l], blk[1]), (pool_s_in[l], blk[2])], (yg, pooled),
            n_seq_blocks=n_sample // SAMPLE_SEQS, n_time_blocks=1, rows=sample_block_rows,
            seqs=SAMPLE_SEQS, row_block0=sample_block0, n_real=None, front=0, pos0=PAST_LEN,
            name="seq_sample")
        conv_s.append(s_conv)
        pool_s.append(s_pool)

        x = _mixout(x, yg, pooled, gates, w_ssd16[l], pool_w16[l], vec(pool_scale)[l], w_pb16[l],
                    w_out16[l], vec(ln_mix_post)[l])
        x = _ffn(x, vec(ln_ffn2_pre)[l], ffn2_in16[l], ffn2_out16[l], vec(ln_ffn2_post)[l])

    y_prompt = x[:prompt_rows].reshape(n_prompt, seq_len, D_MODEL)
    y_sample = x[prompt_rows:prompt_rows + sample_rows].reshape(n_sample, dec_len, D_MODEL)
    state_shape = (DEPTH, -1, N_HEADS, HEAD_DIM, D_STATE)
    return (y_prompt, y_sample, jnp.stack(ssm_p).reshape(state_shape), jnp.stack(conv_p),
            jnp.stack(pool_p), ssm_s.reshape(state_shape), jnp.stack(conv_s), jnp.stack(pool_s))
```

```python
import functools

import jax
import jax.numpy as jnp
from jax import lax
from jax.experimental import pallas as pl
from jax.experimental.pallas import tpu as pltpu

F32 = jnp.float32
BF16 = jnp.bfloat16

D_MODEL = 1024
DEPTH = 2
N_META = 16
D_INNER = 2048
HEAD_DIM = 64
N_HEADS = 32
N_GROUPS = 4
HEADS_PER_GROUP = N_HEADS // N_GROUPS
D_STATE = 128
CONV_K = 4
CONV_DIM = D_INNER + 2 * N_GROUPS * D_STATE
POOL_WINDOWS = (2, 4, 8, 16)
POOL_DIM = 1024
POOL_GROUP_DIM = 256
MAX_WIN = 16
D_FF = 2816
EPS = 1e-6
PAST_LEN = 16384

LANES = 128
SUBLANES = 8
VMEM_LIMIT = 56 * 1024 * 1024

ROW_TILE = 512
SEQ_BLOCK = 128
SSD_ROWS = 512
SAMPLE_SEQS = 8
CONV_COLS = 512
GROUP_COLS = HEADS_PER_GROUP * HEAD_DIM


def _rms(x, g):
    return x * lax.rsqrt(jnp.mean(x * x, axis=-1, keepdims=True) + EPS) * g


def _dot(a, b):
    return jnp.dot(a, b, preferred_element_type=F32)


def _dot_nt(a, b):
    return lax.dot_general(a, b, (((1,), (1,)), ((), ())), preferred_element_type=F32)


def _dot_tn(a, b):
    return lax.dot_general(a, b, (((0,), (0,)), ((), ())), preferred_element_type=F32)


def _dot_exact(a, b):
    return jnp.dot(a, b, precision=lax.Precision.HIGHEST, preferred_element_type=F32)


def _resident(shape, layer=None):
    nd = len(shape)
    if layer is None:
        return pl.BlockSpec(shape, lambda *_: (0,) * nd, pipeline_mode=pl.Buffered(1))
    return pl.BlockSpec((None,) + tuple(shape), lambda *_: (layer,) + (0,) * nd,
                        pipeline_mode=pl.Buffered(1))


def _params(n_axes):
    return pltpu.CompilerParams(dimension_semantics=("arbitrary",) * n_axes,
                                vmem_limit_bytes=VMEM_LIMIT)


def _any():
    return pl.BlockSpec(memory_space=pl.ANY)


def _ffn_body(*refs, in_tiles, out_tiles):
    n_in, n_out = len(in_tiles), len(out_tiles)
    x_refs = refs[:n_in]
    gpre_ref, win_ref, wout_ref, gpost_ref = refs[n_in:n_in + 4]
    o_refs = refs[n_in + 4:]
    i = pl.program_id(0)
    x = x_refs[-1][...]
    for k in range(n_in - 2, -1, -1):
        x = jnp.where(i < sum(in_tiles[:k + 1]), x_refs[k][...], x)
    h = _rms(x, gpre_ref[...]).astype(BF16)
    gate = _dot(h, win_ref[:, :D_FF])
    up = _dot(h, win_ref[:, D_FF:])
    a = (jax.nn.silu(gate) * up).astype(BF16)
    f = _dot(a, wout_ref[...])
    out = x + 0.5 * _rms(f, gpost_ref[...])
    if n_out == 1:
        o_refs[0][...] = out
    else:
        start = 0
        for k in range(n_out):
            @pl.when((i >= start) & (i < start + out_tiles[k]))
            def _(k=k):
                o_refs[k][...] = out
            start += out_tiles[k]


def _part_spec(tiles_before, n_tiles):
    return pl.BlockSpec((ROW_TILE, D_MODEL),
                        lambda i: (jnp.clip(i - tiles_before, 0, n_tiles - 1), 0))


def _ffn(x_parts, l, gpre, w_in, w_out, gpost, out_rows=None):
    in_tiles = [p.shape[0] // ROW_TILE for p in x_parts]
    n_tiles = sum(in_tiles)
    out_rows = out_rows or [n_tiles * ROW_TILE]
    out_tiles = [r // ROW_TILE for r in out_rows]
    in_specs = [_part_spec(sum(in_tiles[:k]), t) for k, t in enumerate(in_tiles)]
    out_specs = [_part_spec(sum(out_tiles[:k]), t) for k, t in enumerate(out_tiles)]
    res = pl.pallas_call(
        functools.partial(_ffn_body, in_tiles=in_tiles, out_tiles=out_tiles),
        grid=(sum(out_tiles),),
        in_specs=in_specs + [_resident((1, D_MODEL), l), _resident((D_MODEL, 2 * D_FF), l),
                             _resident((D_FF, D_MODEL), l), _resident((1, D_MODEL), l)],
        out_specs=out_specs,
        out_shape=[jax.ShapeDtypeStruct((r, D_MODEL), F32) for r in out_rows],
        compiler_params=_params(1),
        name="ffn",
    )(*x_parts, gpre, w_in, w_out, gpost)
    return res[0] if len(res) == 1 else res


_C_XBC = CONV_DIM
_C_U = _C_XBC + POOL_DIM
_C_Z = _C_U + D_INNER
_C_G = _C_Z + 2 * D_MODEL


def _window_means_minus_token(full, ts, cnt=None):
    outs = []
    for k, w in enumerate(POOL_WINDOWS):
        fk = full[:, k * POOL_GROUP_DIM:(k + 1) * POOL_GROUP_DIM]
        acc = fk
        sh = 1
        while sh < w:
            acc = acc + pltpu.roll(acc, sh, axis=0)
            sh *= 2
        wsum = acc[MAX_WIN:MAX_WIN + ts]
        mean = wsum * (1.0 / w) if cnt is None else wsum / cnt[k]
        outs.append(mean - fk[MAX_WIN:MAX_WIN + ts])
    return jnp.concatenate(outs, axis=1)


def _inproj_rest_body(x_ref, g_ref, w_ref, wdt_ref, xbc_ref, u_ref, z_ref, gate_ref, dt_ref):
    h = _rms(x_ref[...], g_ref[...]).astype(BF16)
    xbc_ref[...] = _dot(h, w_ref[:, :_C_XBC]).astype(BF16)
    u_ref[...] = _dot(h, w_ref[:, _C_XBC:_C_U]).astype(BF16)
    z_ref[...] = _dot(h, w_ref[:, _C_U:_C_Z]).astype(BF16)
    gate_ref[...] = _dot(h, w_ref[:, _C_Z:_C_G]).astype(BF16)
    dt_ref[...] = _dot(h, wdt_ref[...])


def _inproj_rest(x, l, g, w_main, w_dt, tile0, n_tiles):
    n = x.shape[0]

    def flat(c):
        return pl.BlockSpec((ROW_TILE, c), lambda i: (tile0 + i, 0))

    def local(c):
        return pl.BlockSpec((ROW_TILE, c), lambda i: (i, 0))

    rest = n_tiles * ROW_TILE
    return pl.pallas_call(
        _inproj_rest_body,
        grid=(n_tiles,),
        in_specs=[flat(D_MODEL), _resident((1, D_MODEL), l), _resident((D_MODEL, _C_G), l),
                  _resident((D_MODEL, LANES), l)],
        out_specs=[local(CONV_DIM), local(POOL_DIM), flat(D_INNER), flat(2 * D_MODEL), flat(LANES)],
        out_shape=[jax.ShapeDtypeStruct((rest, CONV_DIM), BF16),
                   jax.ShapeDtypeStruct((rest, POOL_DIM), BF16),
                   jax.ShapeDtypeStruct((n, D_INNER), BF16),
                   jax.ShapeDtypeStruct((n, 2 * D_MODEL), BF16),
                   jax.ShapeDtypeStruct((n, LANES), F32)],
        compiler_params=_params(1),
        name="inproj_rest",
    )(x, g, w_main, w_dt)


def _inproj_prompt_body(x_ref, g_ref, w_ref, wdt_ref, convw_ref, convb_ref, conv0_ref, pool0_ref,
                        z_in, gate_in, dt_in, pooled_in,
                        xact_ref, z_ref, gate_ref, dt_ref, pooled_ref, conv_ref, pool_ref,
                        ctail, ptail):
    del z_in, gate_in, dt_in, pooled_in
    j = pl.program_id(1)

    @pl.when(j == 0)
    def _():
        ctail[0:SUBLANES - (CONV_K - 1), :] = jnp.zeros((SUBLANES - (CONV_K - 1), CONV_DIM), F32)
        ctail[SUBLANES - (CONV_K - 1):SUBLANES, :] = conv0_ref[0]
        ptail[0:1, :] = jnp.zeros((1, POOL_DIM), F32)
        ptail[1:MAX_WIN, :] = pool0_ref[0]

    h = _rms(x_ref[...], g_ref[...]).astype(BF16)

    for c0 in range(0, CONV_DIM, CONV_COLS):
        cs = slice(c0, c0 + CONV_COLS)
        xbc = _dot(h, w_ref[:, cs])
        full = jnp.concatenate([ctail[:, cs], xbc], axis=0)
        acc = convb_ref[:, cs] + convw_ref[CONV_K - 1:CONV_K, cs] * xbc
        for k in range(CONV_K - 1):
            lo = SUBLANES - (CONV_K - 1) + k
            acc = acc + convw_ref[k:k + 1, cs] * full[lo:lo + ROW_TILE]
        xact_ref[:, cs] = jax.nn.silu(acc).astype(BF16)
        ctail[:, cs] = xbc[ROW_TILE - SUBLANES:]
    conv_ref[0] = ctail[SUBLANES - (CONV_K - 1):SUBLANES, :]

    u = _dot(h, w_ref[:, _C_XBC:_C_U])
    fullp = jnp.concatenate([ptail[...], u], axis=0)
    pooled_ref[...] = _window_means_minus_token(fullp, ROW_TILE).astype(BF16)
    ptail[...] = fullp[ROW_TILE:]
    pool_ref[0] = fullp[ROW_TILE + 1:]

    z_ref[...] = _dot(h, w_ref[:, _C_U:_C_Z]).astype(BF16)
    gate_ref[...] = _dot(h, w_ref[:, _C_Z:_C_G]).astype(BF16)
    dt_ref[...] = _dot(h, wdt_ref[...])


def _inproj_prompt(x, l, g, w_main, w_dt, conv_w, conv_b, conv0, pool0, flats, n_seq, tiles_per_seq):
    rows = n_seq * tiles_per_seq * ROW_TILE

    def row(c):
        return pl.BlockSpec((ROW_TILE, c), lambda b, j: (b * tiles_per_seq + j, 0))

    def const(shape):
        nd = len(shape)
        return pl.BlockSpec(shape, lambda b, j: (0,) * nd)

    z, gates, dt, pooled = flats
    n = z.shape[0]
    return pl.pallas_call(
        _inproj_prompt_body,
        grid=(n_seq, tiles_per_seq),
        in_specs=[row(D_MODEL), _resident((1, D_MODEL), l), _resident((D_MODEL, _C_G), l),
                  _resident((D_MODEL, LANES), l), _resident((CONV_K, CONV_DIM), l),
                  _resident((1, CONV_DIM), l),
                  const((1, CONV_K - 1, CONV_DIM)), const((1, MAX_WIN - 1, POOL_DIM)),
                  _any(), _any(), _any(), _any()],
        out_specs=[row(CONV_DIM), row(D_INNER), row(2 * D_MODEL), row(LANES), row(POOL_DIM),
                   pl.BlockSpec((1, CONV_K - 1, CONV_DIM), lambda b, j: (b, 0, 0)),
                   pl.BlockSpec((1, MAX_WIN - 1, POOL_DIM), lambda b, j: (b, 0, 0))],
        out_shape=[jax.ShapeDtypeStruct((rows, CONV_DIM), BF16),
                   jax.ShapeDtypeStruct((n, D_INNER), BF16),
                   jax.ShapeDtypeStruct((n, 2 * D_MODEL), BF16),
                   jax.ShapeDtypeStruct((n, LANES), F32),
                   jax.ShapeDtypeStruct((n, POOL_DIM), BF16),
                   jax.ShapeDtypeStruct((n_seq, CONV_K - 1, CONV_DIM), F32),
                   jax.ShapeDtypeStruct((n_seq, MAX_WIN - 1, POOL_DIM), F32)],
        scratch_shapes=[pltpu.VMEM((SUBLANES, CONV_DIM), F32), pltpu.VMEM((MAX_WIN, POOL_DIM), F32)],
        input_output_aliases={8: 1, 9: 2, 10: 3, 11: 4},
        compiler_params=_params(2),
        name="inproj_prompt",
    )(x, g, w_main, w_dt, conv_w, conv_b, conv0, pool0, z, gates, dt, pooled)


def _mixout_body(x_ref, y_ref, z_ref, pooled_ref, gate_ref, snorm_ref, wssd_ref, poolw_ref,
                 pscale_ref, wpb_ref, wout_ref, gpost_ref, o_ref):
    y = y_ref[...].astype(F32) * jax.nn.silu(z_ref[...].astype(F32))
    ssd_out = _dot(_rms(y, snorm_ref[...]).astype(BF16), wssd_ref[...])
    pooled = pooled_ref[...]
    mixed = jnp.concatenate(
        [_dot(pooled[:, k * POOL_GROUP_DIM:(k + 1) * POOL_GROUP_DIM], poolw_ref[k])
         for k in range(len(POOL_WINDOWS))], axis=1) * pscale_ref[...]
    pool_out = _dot(mixed.astype(BF16), wpb_ref[...])
    gates = gate_ref[...].astype(F32)
    merged = (jax.nn.sigmoid(gates[:, :D_MODEL]) * ssd_out
              + jax.nn.sigmoid(gates[:, D_MODEL:]) * pool_out)
    m = _dot(merged.astype(BF16), wout_ref[...])
    o_ref[...] = x_ref[...] + _rms(m, gpost_ref[...])


def _mixout(x, y, z, pooled, gates, l, snorm, w_ssd, pool_w, pool_scale, w_pb, w_out, gpost, n_tiles):
    def row(c):
        return pl.BlockSpec((ROW_TILE, c), lambda i: (i, 0))

    return pl.pallas_call(
        _mixout_body,
        grid=(n_tiles,),
        in_specs=[row(D_MODEL), row(D_INNER), row(D_INNER), row(POOL_DIM), row(2 * D_MODEL),
                  _resident((1, D_INNER), l), _resident((D_INNER, D_MODEL), l),
                  _resident((len(POOL_WINDOWS), POOL_GROUP_DIM, POOL_GROUP_DIM), l),
                  _resident((1, POOL_DIM), l), _resident((POOL_DIM, D_MODEL), l),
                  _resident((D_MODEL, D_MODEL), l), _resident((1, D_MODEL), l)],
        out_specs=row(D_MODEL),
        out_shape=jax.ShapeDtypeStruct((n_tiles * ROW_TILE, D_MODEL), F32),
        compiler_params=_params(1),
        name="mixout",
    )(x, y, z, pooled, gates, snorm, w_ssd, pool_w, pool_scale, w_pb, w_out, gpost)


def _expand_heads(q, e3):
    b1 = q.astype(BF16).astype(F32)
    r1 = q - b1
    b2 = r1.astype(BF16).astype(F32)
    b3 = (r1 - b2).astype(BF16).astype(F32)
    lane = lax.broadcasted_iota(jnp.int32, (1, LANES), 1)
    packed = jnp.where(lane < N_HEADS, b1,
                       jnp.where(lane < 2 * N_HEADS, pltpu.roll(b2, N_HEADS, axis=1),
                                 jnp.where(lane < 3 * N_HEADS, pltpu.roll(b3, 2 * N_HEADS, axis=1),
                                           0.0)))
    return _dot(packed.astype(BF16), e3)


def _ssd_block(xs, b16, c16, bm, cm, dt_raw, valid, dtb_ref, alog_ref, dskip_ref, e3_ref, ssm_ref,
               rows, seqs):
    ts = rows // seqs
    xdt = dt_raw + dtb_ref[...]
    dt = jnp.maximum(xdt, 0.0) + jnp.log1p(jnp.exp(-jnp.abs(xdt)))
    if valid is not None:
        dt = jnp.where(valid, dt, 0.0)
    dta = dt * (-jnp.exp(alog_ref[...]))
    r = lax.broadcasted_iota(jnp.int32, (rows, rows), 0)
    c = lax.broadcasted_iota(jnp.int32, (rows, rows), 1)
    if seqs > 1:
        shift = ts.bit_length() - 1
        same = (r >> shift) == (c >> shift)
        causal = (r >= c) & same
        tot = _dot_exact(same.astype(F32), dta)
    else:
        causal = r >= c
        tot = _dot_exact(jnp.ones((rows, rows), F32), dta)
    acum = _dot_exact(causal.astype(F32), dta)
    acum_t = acum.T
    q = jnp.concatenate([dt, jnp.exp(acum), dt * jnp.exp(tot - acum)], axis=0)
    qe = _expand_heads(q, e3_ref[...])
    dt_e, eacum_e, w_e = qe[:rows], qe[rows:2 * rows], qe[2 * rows:]
    xdt16 = (xs * dt_e).astype(BF16)
    xw = xs * w_e

    low = lax.broadcasted_iota(jnp.int32, (1, LANES), 1) < HEAD_DIM
    tiles = []
    for g in range(N_GROUPS):
        cbm = _dot_nt(c16[:, g * D_STATE:(g + 1) * D_STATE], b16[:, g * D_STATE:(g + 1) * D_STATE])
        for hh in range(0, HEADS_PER_GROUP, 2):
            h0 = g * HEADS_PER_GROUP + hh
            t = h0 // 2
            xt = xdt16[:, t * LANES:(t + 1) * LANES]
            zero = jnp.zeros_like(xt)
            rhs = jnp.concatenate([jnp.where(low, xt, zero), jnp.where(low, zero, xt)], axis=0)
            ms = []
            for h in (h0, h0 + 1):
                seg = acum[:, h:h + 1] - acum_t[h:h + 1, :]
                ms.append((cbm * jnp.exp(jnp.where(causal, seg, -jnp.inf))).astype(BF16))
            tiles.append(_dot(jnp.concatenate(ms, axis=1), rhs))
    y = jnp.concatenate(tiles, axis=1)

    inter_parts = []
    for s in range(seqs):
        rs = slice(s * ts, (s + 1) * ts)
        state = ssm_ref[s]
        state16 = state.astype(BF16)
        if seqs > 1:
            c_s, b_s = cm[rs].astype(BF16), bm[rs].astype(BF16)
        else:
            c_s, b_s = c16, b16
        xw_s = xw[rs].astype(BF16)
        inter_parts.append(jnp.concatenate(
            [_dot_nt(c_s[:, g * D_STATE:(g + 1) * D_STATE], state16[g * GROUP_COLS:(g + 1) * GROUP_COLS, :])
             for g in range(N_GROUPS)], axis=1))
        cdec = jnp.exp(tot[s * ts:s * ts + 1, :])
        new_rows = []
        for g in range(N_GROUPS):
            inc = _dot_tn(xw_s[:, g * GROUP_COLS:(g + 1) * GROUP_COLS], b_s[:, g * D_STATE:(g + 1) * D_STATE])
            for hh in range(HEADS_PER_GROUP):
                h = g * HEADS_PER_GROUP + hh
                new_rows.append(state[h * HEAD_DIM:(h + 1) * HEAD_DIM, :] * cdec[:, h:h + 1]
                                + inc[hh * HEAD_DIM:(hh + 1) * HEAD_DIM, :])
        ssm_ref[s] = jnp.concatenate(new_rows, axis=0)
    y_inter = jnp.concatenate(inter_parts, axis=0) if seqs > 1 else inter_parts[0]
    return y + y_inter * eacum_e + dskip_ref[...] * xs


def _ssd_prompt_body(xact_ref, dt_ref, dtb_ref, alog_ref, dskip_ref, e3_ref, ssm0_ref, y_in,
                     y_ref, ssm_ref):
    del y_in
    @pl.when(pl.program_id(1) == 0)
    def _():
        ssm_ref[...] = ssm0_ref[...]

    for c0 in range(0, SSD_ROWS, SEQ_BLOCK):
        rs = slice(c0, c0 + SEQ_BLOCK)
        xs = xact_ref[rs, :D_INNER].astype(F32)
        b16 = xact_ref[rs, D_INNER:D_INNER + N_GROUPS * D_STATE]
        c16 = xact_ref[rs, D_INNER + N_GROUPS * D_STATE:]
        y = _ssd_block(xs, b16, c16, None, None, dt_ref[rs, :], None, dtb_ref, alog_ref, dskip_ref,
                       e3_ref, ssm_ref, SEQ_BLOCK, 1)
        y_ref[rs, :] = y.astype(BF16)


def _ssd_prompt(xact, dt, prm, l, ssm0, y_flat, n_seq, blocks_per_seq):
    def row(c):
        return pl.BlockSpec((SSD_ROWS, c), lambda b, j: (b * blocks_per_seq + j, 0))

    def const(shape):
        nd = len(shape)
        return pl.BlockSpec(shape, lambda b, j: (0,) * nd)

    return pl.pallas_call(
        _ssd_prompt_body,
        grid=(n_seq, blocks_per_seq),
        in_specs=[row(CONV_DIM), row(LANES), _resident((1, LANES), l), _resident((1, LANES), l),
                  _resident((1, D_INNER), l), _resident((LANES, D_INNER)),
                  const((1, D_INNER, D_STATE)), _any()],
        out_specs=[row(D_INNER), pl.BlockSpec((1, D_INNER, D_STATE), lambda b, j: (b, 0, 0))],
        out_shape=[jax.ShapeDtypeStruct(y_flat.shape, BF16),
                   jax.ShapeDtypeStruct((n_seq, D_INNER, D_STATE), F32)],
        input_output_aliases={7: 0},
        compiler_params=_params(2),
        name="ssd_prompt",
    )(xact, dt, prm["dt_bias"], prm["a_log"], prm["d_skip"], prm["e3"], ssm0, y_flat)


def _seq_rest_body(xbc_ref, u_ref, dt_ref, convw_ref, convb_ref, dtb_ref, alog_ref, dskip_ref,
                   e3_ref, ssm0_ref, conv0_ref, pool0_ref, *rest,
                   rows, seqs, n_real, front, n_alias):
    y_ref, pooled_ref, ssm_ref, conv_ref, pool_ref, cext, pext = rest[n_alias:]
    ts = rows // seqs
    j = pl.program_id(1)
    tail0 = SUBLANES - (CONV_K - 1)

    @pl.when(j == 0)
    def _():
        ssm_ref[...] = ssm0_ref[...]
        cext[:, tail0:SUBLANES, :] = conv0_ref[...]
        pext[:, 0:1, :] = jnp.zeros((seqs, 1, POOL_DIM), F32)
        pext[:, 1:MAX_WIN, :] = pool0_ref[...]

    def step():
        row_id = lax.broadcasted_iota(jnp.int32, (rows, 1), 0)
        xbc = xbc_ref[...].astype(F32)
        u = u_ref[...].astype(F32)
        valid = None
        if front:
            valid = row_id >= front
            xbc = jnp.where(valid, xbc, 0.0)
            u = jnp.where(valid, u, 0.0)

        conv_parts = []
        for s in range(seqs):
            cext[s, SUBLANES:SUBLANES + ts, :] = xbc[s * ts:(s + 1) * ts, :]
            acc = convb_ref[...]
            for k in range(CONV_K):
                acc = acc + convw_ref[k:k + 1, :] * cext[s, tail0 + k:tail0 + k + ts, :]
            conv_parts.append(acc)
            tail = cext[s, SUBLANES + ts - (CONV_K - 1):SUBLANES + ts, :]
            cext[s, tail0:SUBLANES, :] = tail
            conv_ref[s] = tail
        xact = jax.nn.silu(jnp.concatenate(conv_parts, axis=0) if seqs > 1 else conv_parts[0])
        xs = xact[:, :D_INNER]
        bm = xact[:, D_INNER:D_INNER + N_GROUPS * D_STATE]
        cm = xact[:, D_INNER + N_GROUPS * D_STATE:]
        y = _ssd_block(xs, bm.astype(BF16), cm.astype(BF16), bm, cm, dt_ref[...], valid, dtb_ref,
                       alog_ref, dskip_ref, e3_ref, ssm_ref, rows, seqs)
        y_ref[...] = y.astype(BF16)

        cnt = None
        if front:
            pos = jnp.maximum(row_id - front, 0)
            cnt = [jnp.minimum(pos + 1, w).astype(F32) for w in POOL_WINDOWS]
        pooled_parts = []
        for s in range(seqs):
            pext[s, MAX_WIN:MAX_WIN + ts, :] = u[s * ts:(s + 1) * ts, :]
            full = pext[s]
            pooled_parts.append(_window_means_minus_token(full, ts, cnt))
            new_tail = full[ts:ts + MAX_WIN]
            pext[s, 0:MAX_WIN, :] = new_tail
            pool_ref[s] = new_tail[1:]
        pooled = jnp.concatenate(pooled_parts, axis=0) if seqs > 1 else pooled_parts[0]
        pooled_ref[...] = pooled.astype(BF16)

    if n_real is None:
        step()
    else:
        pl.when(j < n_real)(step)

        @pl.when(j >= n_real)
        def _():
            y_ref[...] = jnp.zeros_like(y_ref)
            pooled_ref[...] = jnp.zeros_like(pooled_ref)


def _seq_rest(xbc, u, dt, prm, l, states, flats, ssm_out, *, n_rows, n_seq_blocks, n_time_blocks,
              rows, seqs, local_block0, flat_block0, n_real, front, name):
    def local(c):
        return pl.BlockSpec((rows, c), lambda b, j: (local_block0 + b * n_time_blocks + j, 0))

    def flat(c):
        return pl.BlockSpec((rows, c), lambda b, j: (flat_block0 + b * n_time_blocks + j, 0))

    (ssm0, ssm0_spec), (conv0, conv0_spec), (pool0, pool0_spec) = states
    n_b = n_seq_blocks * seqs
    in_specs = [local(CONV_DIM), local(POOL_DIM), flat(LANES),
                _resident((CONV_K, CONV_DIM), l), _resident((1, CONV_DIM), l),
                _resident((1, LANES), l), _resident((1, LANES), l), _resident((1, D_INNER), l),
                _resident((LANES, D_INNER)), ssm0_spec, conv0_spec, pool0_spec]
    args = [xbc, u, dt, prm["conv_w"], prm["conv_b"], prm["dt_bias"], prm["a_log"],
            prm["d_skip"], prm["e3"], ssm0, conv0, pool0]
    aliases = {}
    n_alias = 0
    if flats is not None:
        for k, buf in enumerate(flats):
            aliases[len(args)] = k
            args.append(buf)
            in_specs.append(_any())
            n_alias += 1
    ssm_shape, ssm_spec, ssm_buf, conv_spec, pool_spec = ssm_out
    if ssm_buf is not None:
        aliases[len(args)] = 2
        args.append(ssm_buf)
        in_specs.append(_any())
        n_alias += 1
    ts = rows // seqs
    return pl.pallas_call(
        functools.partial(_seq_rest_body, rows=rows, seqs=seqs, n_real=n_real, front=front,
                          n_alias=n_alias),
        grid=(n_seq_blocks, n_time_blocks),
        in_specs=in_specs,
        out_specs=[flat(D_INNER), flat(POOL_DIM), ssm_spec, conv_spec, pool_spec],
        out_shape=[jax.ShapeDtypeStruct((n_rows, D_INNER), BF16),
                   jax.ShapeDtypeStruct((n_rows, POOL_DIM), BF16),
                   jax.ShapeDtypeStruct(ssm_shape, F32),
                   jax.ShapeDtypeStruct((n_b, CONV_K - 1, CONV_DIM), F32),
                   jax.ShapeDtypeStruct((n_b, MAX_WIN - 1, POOL_DIM), F32)],
        scratch_shapes=[pltpu.VMEM((seqs, ts + SUBLANES, CONV_DIM), F32),
                        pltpu.VMEM((seqs, ts + MAX_WIN, POOL_DIM), F32)],
        input_output_aliases=aliases,
        compiler_params=_params(2),
        name=name,
    )(*args)


def _state_specs(n_seqs_per_block, index):
    s = n_seqs_per_block
    return (pl.BlockSpec((s, D_INNER, D_STATE), lambda b, j: (index(b), 0, 0)),
            pl.BlockSpec((s, CONV_K - 1, CONV_DIM), lambda b, j: (index(b), 0, 0)),
            pl.BlockSpec((s, MAX_WIN - 1, POOL_DIM), lambda b, j: (index(b), 0, 0)))


def kernel(x_prompt, x_sample, state_ssm, state_conv, state_pool, meta_tokens, ln_ffn1_pre, ln_ffn1_post,
           ffn1_w_in, ffn1_w_out, ln_mix_pre, ln_mix_post, w_in, conv_w, conv_b, dt_bias, a_log, d_skip,
           ssd_norm, w_ssd_branch, pool_w, pool_scale, w_pool_branch, w_out, ln_ffn2_pre, ln_ffn2_post,
           ffn2_w_in, ffn2_w_out):
    n_prompt, seq_len, _ = x_prompt.shape
    n_sample, dec_len, _ = x_sample.shape
    prompt_rows = n_prompt * seq_len
    sample_rows = n_sample * dec_len
    meta_front = SEQ_BLOCK - N_META
    sample_block_rows = SAMPLE_SEQS * dec_len
    assert seq_len % ROW_TILE == 0 and sample_rows % ROW_TILE == 0 and n_sample % SAMPLE_SEQS == 0
    assert ROW_TILE % SEQ_BLOCK == 0 and ROW_TILE % sample_block_rows == 0 and seq_len % SSD_ROWS == 0
    assert N_META + 1 >= MAX_WIN and PAST_LEN + 1 >= MAX_WIN
    prompt_tiles = prompt_rows // ROW_TILE
    sample_tiles = sample_rows // ROW_TILE
    rest_tiles = sample_tiles + 1
    total_rows = (prompt_tiles + rest_tiles) * ROW_TILE

    x_rest = jnp.concatenate([
        x_sample.reshape(sample_rows, D_MODEL), jnp.zeros((meta_front, D_MODEL), F32),
        meta_tokens.astype(F32), jnp.zeros((ROW_TILE - SEQ_BLOCK, D_MODEL), F32)], axis=0)

    o1 = D_INNER
    o2 = o1 + CONV_DIM
    o3 = o2 + N_HEADS
    o4 = o3 + POOL_DIM
    w_main = jnp.concatenate([w_in[..., o1:o2], w_in[..., o3:o4], w_in[..., :o1], w_in[..., o4:]],
                             axis=-1).astype(BF16)
    w_dt = jnp.pad(w_in[..., o2:o3], ((0, 0), (0, 0), (0, LANES - N_HEADS))).astype(BF16)
    head_of_lane = jnp.arange(LANES) % N_HEADS
    e3 = ((head_of_lane[:, None] == (jnp.arange(D_INNER) // HEAD_DIM)[None, :])
          & (jnp.arange(LANES) < 3 * N_HEADS)[:, None]).astype(BF16)

    def vec(p, width=None):
        p = p.astype(F32)
        if width is not None:
            p = jnp.pad(p, ((0, 0), (0, width - p.shape[-1])))
        return p[:, None, :]

    prm = {
        "conv_w": conv_w.astype(F32), "conv_b": vec(conv_b), "dt_bias": vec(dt_bias, LANES),
        "a_log": vec(a_log, LANES), "d_skip": vec(jnp.repeat(d_skip, HEAD_DIM, axis=-1)), "e3": e3,
    }
    ffn1_in16, ffn1_out16 = ffn1_w_in.astype(BF16), ffn1_w_out.astype(BF16)
    ffn2_in16, ffn2_out16 = ffn2_w_in.astype(BF16), ffn2_w_out.astype(BF16)
    w_ssd16, pool_w16 = w_ssd_branch.astype(BF16), pool_w.astype(BF16)
    w_pb16, w_out16 = w_pool_branch.astype(BF16), w_out.astype(BF16)

    ssm_s_in = state_ssm.reshape(DEPTH, n_sample, D_INNER, D_STATE).astype(F32)
    conv_s_in = state_conv.astype(F32)
    pool_s_in = state_pool.astype(F32)
    zero_states = (jnp.zeros((1, D_INNER, D_STATE), F32), jnp.zeros((1, CONV_K - 1, CONV_DIM), F32),
                   jnp.zeros((1, MAX_WIN - 1, POOL_DIM), F32))

    tiles_per_seq = seq_len // ROW_TILE
    blocks_per_seq = seq_len // SSD_ROWS
    meta_local_block0 = sample_rows // SEQ_BLOCK
    meta_flat_block0 = (prompt_rows + sample_rows) // SEQ_BLOCK
    meta_blocks = ROW_TILE // SEQ_BLOCK
    sample_flat_block0 = prompt_rows // sample_block_rows

    ssm_p, conv_p, pool_p, conv_s, pool_s = [], [], [], [], []
    ssm_s = None
    x = None
    for l in range(DEPTH):
        last = l == DEPTH - 1
        x_parts = [x_prompt.reshape(prompt_rows, D_MODEL), x_rest] if l == 0 else [x]
        x = _ffn(x_parts, l, vec(ln_ffn1_pre), ffn1_in16, ffn1_out16, vec(ln_ffn1_post))

        gmix = vec(ln_mix_pre)
        xbc_r, u_r, z, gates, dt = _inproj_rest(x, l, gmix, w_main, w_dt, prompt_tiles, rest_tiles)

        one = _state_specs(1, lambda b: 0)
        y, pooled, m_ssm, m_conv, m_pool = _seq_rest(
            xbc_r, u_r, dt, prm, l, list(zip(zero_states, one)), None,
            ((1, D_INNER, D_STATE), one[0], None, one[1], one[2]),
            n_rows=total_rows, n_seq_blocks=1, n_time_blocks=meta_blocks, rows=SEQ_BLOCK, seqs=1,
            local_block0=meta_local_block0, flat_block0=meta_flat_block0, n_real=1, front=meta_front,
            name="seq_meta")

        xact, z, gates, dt, pooled, p_conv, p_pool = _inproj_prompt(
            x, l, gmix, w_main, w_dt, prm["conv_w"], prm["conv_b"], m_conv, m_pool,
            (z, gates, dt, pooled), n_prompt, tiles_per_seq)
        y, p_ssm = _ssd_prompt(xact, dt, prm, l, m_ssm, y, n_prompt, blocks_per_seq)
        ssm_p.append(p_ssm)
        conv_p.append(p_conv)
        pool_p.append(p_pool)

        blk = _state_specs(SAMPLE_SEQS, lambda b: b)

        def layer_spec(*tail, l=l):
            return pl.BlockSpec((None, SAMPLE_SEQS) + tail, lambda b, j: (l, b, 0, 0))

        layer_ssm_spec = layer_spec(D_INNER, D_STATE)
        y, pooled, ssm_s, s_conv, s_pool = _seq_rest(
            xbc_r, u_r, dt, prm, l,
            [(ssm_s_in, layer_ssm_spec), (conv_s_in, layer_spec(CONV_K - 1, CONV_DIM)),
             (pool_s_in, layer_spec(MAX_WIN - 1, POOL_DIM))], (y, pooled),
            ((DEPTH, n_sample, D_INNER, D_STATE), layer_ssm_spec, ssm_s, blk[1], blk[2]),
            n_rows=total_rows, n_seq_blocks=n_sample // SAMPLE_SEQS, n_time_blocks=1,
            rows=sample_block_rows, seqs=SAMPLE_SEQS, local_block0=0, flat_block0=sample_flat_block0,
            n_real=None, front=0, name="seq_sample")
        conv_s.append(s_conv)
        pool_s.append(s_pool)

        live_tiles = prompt_tiles + (sample_tiles if last else rest_tiles)
        x = _mixout(x, y, z, pooled, gates, l, vec(ssd_norm), w_ssd16, pool_w16, vec(pool_scale),
                    w_pb16, w_out16, vec(ln_mix_post), live_tiles)
        x = _ffn([x], l, vec(ln_ffn2_pre), ffn2_in16, ffn2_out16, vec(ln_ffn2_post),
                 out_rows=[prompt_rows, sample_rows] if last else None)

    y_prompt = x[0].reshape(n_prompt, seq_len, D_MODEL)
    y_sample = x[1].reshape(n_sample, dec_len, D_MODEL)
    state_shape = (DEPTH, -1, N_HEADS, HEAD_DIM, D_STATE)
    return (y_prompt, y_sample, jnp.stack(ssm_p).reshape(state_shape), jnp.stack(conv_p),
            jnp.stack(pool_p), ssm_s.reshape(state_shape), jnp.stack(conv_s), jnp.stack(pool_s))
```
